```python
import math
import jax, jax.numpy as jnp
from jax import lax
import numpy as np

D_MODEL = 1024
BATCH = 2
SEQ = 16384
DEPTH = 4
DEC_BATCH = 16
DEC_SEQ = 32
PAST_LEN = 1024

CHUNK = 64
N_HEADS = 8
HEAD_DIM = 64
ATT_W = N_HEADS * 2 * HEAD_DIM
CONV_W = D_MODEL // 2
CONV_K = 3
POOL_W = D_MODEL // 2
POOL_WINDOWS = (2, 4, 8, 16)
POOL_GROUPS = len(POOL_WINDOWS)
POOL_GW = POOL_W // POOL_GROUPS
POOL_HIST = max(POOL_WINDOWS) - 1
N_BRANCH = 3
IN_COLS = 3 * ATT_W + 3 * CONV_W + POOL_W + N_BRANCH * D_MODEL
BRANCH_ROWS = ATT_W + CONV_W + POOL_W
D_FF = 2816
PLE_DIM = 256
ROPE_THETA = 10000.0
Q_BLOCK = 128
RMS_EPS = 1e-6
N_NORMS = 8

kernel_name = 'hybrid_streaming_encoder_step'


def rms_norm(x, g):
    xf = x.astype(jnp.float32)
    y = xf * lax.rsqrt(jnp.mean(xf * xf, axis=-1, keepdims=True) + RMS_EPS)
    return (y * g.astype(jnp.float32)).astype(x.dtype)


def swiglu(x, w_up, w_down):
    gate, up = jnp.split(x @ w_up, 2, axis=-1)
    return (jax.nn.silu(gate) * up) @ w_down


def rope(x, pos):
    half = HEAD_DIM // 2
    inv = ROPE_THETA ** (-jnp.arange(half, dtype=jnp.float32) / half)
    ang = pos.astype(jnp.float32)[:, None] * inv[None, :]
    cos = jnp.cos(ang)[None, :, None, :]
    sin = jnp.sin(ang)[None, :, None, :]
    xf = x.astype(jnp.float32)
    x1, x2 = xf[..., :half], xf[..., half:]
    return jnp.concatenate([x1 * cos - x2 * sin, x2 * cos + x1 * sin], axis=-1).astype(x.dtype)


def diff_attn_core(q, k, v, lam, mask):
    s = jnp.einsum('bqhd,bkhd->bhqk', q, k).astype(jnp.float32) * (HEAD_DIM ** -0.5)
    if mask is not None:
        s = jnp.where(mask[None, None], s, -jnp.inf)
    p = jax.nn.softmax(s, axis=-1)
    b, _, tq, tk = p.shape
    p = p.reshape(b, N_HEADS, 2, tq, tk)
    a = (p[:, :, 0] - lam * p[:, :, 1]).astype(v.dtype)
    return jnp.einsum('bhqk,bkhe->bqhe', a, v)


def prompt_attention(q, k, v, lam):
    b, s = q.shape[:2]
    nb = s // Q_BLOCK
    qb = q.reshape(b, nb, Q_BLOCK, 2 * N_HEADS, HEAD_DIM).transpose(1, 0, 2, 3, 4)
    qpos = jnp.arange(s, dtype=jnp.int32).reshape(nb, Q_BLOCK)
    kchunk = jnp.arange(s, dtype=jnp.int32) // CHUNK

    def block(args):
        qblk, pos = args
        mask = (pos[:, None] // CHUNK) >= kchunk[None, :]
        return diff_attn_core(qblk, k, v, lam, mask)

    out = lax.map(block, (qb, qpos))
    return out.transpose(1, 0, 2, 3, 4).reshape(b, s, N_HEADS, 2 * HEAD_DIM)


def short_conv(z, hist, conv_w):
    t = z.shape[1]
    zext = jnp.concatenate([hist, z], axis=1)
    y = sum(conv_w[j] * zext[:, j:j + t] for j in range(CONV_K))
    return y, zext[:, -(CONV_K - 1):]


def pool_mixer(z, hist, hist_valid, pool_w, pool_scale):
    b, t, _ = z.shape
    zext = jnp.concatenate([hist, z], axis=1)
    zf = zext.astype(jnp.float32)
    c = jnp.concatenate([jnp.zeros_like(zf[:, :1]), jnp.cumsum(zf, axis=1)], axis=1)
    zcur = zf[:, POOL_HIST:]
    tpos = jnp.arange(t, dtype=jnp.int32)
    outs = []
    for g, w in enumerate(POOL_WINDOWS):
        sl = slice(g * POOL_GW, (g + 1) * POOL_GW)
        wsum = c[:, POOL_HIST + 1:POOL_HIST + 1 + t, sl] - c[:, POOL_HIST + 1 - w:POOL_HIST + 1 - w + t, sl]
        cnt = jnp.minimum(tpos + 1 + hist_valid, w).astype(jnp.float32)[None, :, None]
        outs.append(wsum / cnt - zcur[:, :, sl])
    d = jnp.stack(outs, axis=2).astype(z.dtype)
    y = jnp.einsum('btgc,gcd->btgd', d, pool_w).reshape(b, t, POOL_W) * pool_scale
    return y, zext[:, -POOL_HIST:]


def encoder_layer(li, x, p, pos, k_cache, v_cache, conv_hist, pool_hist, pool_valid,
                  norm_g, w_ffn_up, w_ffn_down, w_in, lam, subln_g, conv_w, pool_w, pool_scale,
                  w_branch, w_out, w_ple_up, w_ple_gate):
    b, t = x.shape[:2]
    h = x + 0.5 * rms_norm(swiglu(rms_norm(x, norm_g[0]), w_ffn_up[0], w_ffn_down[0]), norm_g[1])
    u = rms_norm(h, norm_g[2])
    offs = [ATT_W, 2 * ATT_W, 3 * ATT_W, 3 * ATT_W + CONV_W, 3 * ATT_W + 2 * CONV_W,
            3 * ATT_W + 3 * CONV_W, 3 * ATT_W + 3 * CONV_W + POOL_W]
    q, k, v, bg, cg, xin, zp, gates = jnp.split(u @ w_in, offs, axis=-1)
    q = rope(q.reshape(b, t, 2 * N_HEADS, HEAD_DIM), pos)
    k = rope(k.reshape(b, t, 2 * N_HEADS, HEAD_DIM), pos)
    v = v.reshape(b, t, N_HEADS, 2 * HEAD_DIM)
    lam_init = 0.8 - 0.6 * math.exp(-0.3 * li)
    lamf = lam.astype(jnp.float32)
    lam_full = jnp.exp(jnp.sum(lamf[0] * lamf[1])) - jnp.exp(jnp.sum(lamf[2] * lamf[3])) + lam_init
    if k_cache is None:
        att = prompt_attention(q, k, v, lam_full)
    else:
        att = diff_attn_core(q, jnp.concatenate([k_cache, k], axis=1),
                             jnp.concatenate([v_cache, v], axis=1), lam_full, None)
    att = rms_norm(att, subln_g.reshape(N_HEADS, 2 * HEAD_DIM)) * (1.0 - lam_init)
    att = att.reshape(b, t, ATT_W)
    y_conv, conv_new = short_conv(cg * xin, conv_hist, conv_w)
    y_conv = bg * y_conv
    y_pool, pool_new = pool_mixer(zp, pool_hist, pool_valid, pool_w, pool_scale)
    g = jax.nn.sigmoid(gates).reshape(b, t, N_BRANCH, D_MODEL)
    merged = (g[:, :, 0] * (att @ w_branch[:ATT_W])
              + g[:, :, 1] * (y_conv @ w_branch[ATT_W:ATT_W + CONV_W])
              + g[:, :, 2] * (y_pool @ w_branch[ATT_W + CONV_W:]))
    h = h + rms_norm(merged @ w_out, norm_g[3])
    h = h + 0.5 * rms_norm(swiglu(rms_norm(h, norm_g[4]), w_ffn_up[1], w_ffn_down[1]), norm_g[5])
    gate = jax.nn.sigmoid(rms_norm(h, norm_g[6]) @ w_ple_gate)
    h = h + rms_norm((p @ w_ple_up) * gate, norm_g[7])
    return h, k, v, conv_new, pool_new


def setup_inputs(seed: int = 0) -> dict:
    key = jax.random.key(seed)
    ks = jax.random.split(key, 24)
    f32 = jnp.float32

    def nrm(k, shape, scale):
        return jax.random.normal(k, shape, f32) * scale

    return {
        'x_prompt': nrm(ks[0], (BATCH, SEQ, D_MODEL), 1.0),
        'x_sample': nrm(ks[1], (DEC_BATCH, DEC_SEQ, D_MODEL), 1.0),
        'cache_k': nrm(ks[2], (DEPTH, DEC_BATCH, PAST_LEN, 2 * N_HEADS, HEAD_DIM), 1.0),
        'cache_v': nrm(ks[3], (DEPTH, DEC_BATCH, PAST_LEN, N_HEADS, 2 * HEAD_DIM), 1.0),
        'state_conv': nrm(ks[4], (DEPTH, DEC_BATCH, CONV_K - 1, CONV_W), 1.0),
        'state_pool': nrm(ks[5], (DEPTH, DEC_BATCH, POOL_HIST, POOL_W), 1.0),
        'p_prompt': nrm(ks[6], (DEPTH, BATCH, SEQ, PLE_DIM), 1.0),
        'p_sample': nrm(ks[7], (DEPTH, DEC_BATCH, DEC_SEQ, PLE_DIM), 1.0),
        'norm_g': 1.0 + nrm(ks[8], (DEPTH, N_NORMS, D_MODEL), 0.05),
        'w_ffn_up': nrm(ks[9], (DEPTH, 2, D_MODEL, 2 * D_FF), D_MODEL ** -0.5),
        'w_ffn_down': nrm(ks[10], (DEPTH, 2, D_FF, D_MODEL), D_FF ** -0.5),
        'w_in': nrm(ks[11], (DEPTH, D_MODEL, IN_COLS), D_MODEL ** -0.5),
        'lam': nrm(ks[12], (DEPTH, 4, HEAD_DIM), 0.1),
        'subln_g': 1.0 + nrm(ks[13], (DEPTH, ATT_W), 0.05),
        'conv_w': nrm(ks[14], (DEPTH, CONV_K, CONV_W), CONV_K ** -0.5),
        'pool_w': nrm(ks[15], (DEPTH, POOL_GROUPS, POOL_GW, POOL_GW), POOL_GW ** -0.5),
        'pool_scale': 1.0 + nrm(ks[16], (DEPTH, POOL_W), 0.1),
        'w_branch': nrm(ks[17], (DEPTH, BRANCH_ROWS, D_MODEL), ATT_W ** -0.5),
        'w_out': nrm(ks[18], (DEPTH, D_MODEL, D_MODEL), D_MODEL ** -0.5),
        'w_ple_up': nrm(ks[19], (DEPTH, PLE_DIM, D_MODEL), PLE_DIM ** -0.5),
        'w_ple_gate': nrm(ks[20], (DEPTH, D_MODEL, D_MODEL), D_MODEL ** -0.5),
    }


def reference(x_prompt, x_sample, cache_k, cache_v, state_conv, state_pool, p_prompt, p_sample,
              norm_g, w_ffn_up, w_ffn_down, w_in, lam, subln_g, conv_w, pool_w, pool_scale,
              w_branch, w_out, w_ple_up, w_ple_gate):
    b = x_prompt.shape[0]
    pos_p = jnp.arange(x_prompt.shape[1], dtype=jnp.int32)
    pos_s = PAST_LEN + jnp.arange(x_sample.shape[1], dtype=jnp.int32)
    zero_conv = jnp.zeros((b, CONV_K - 1, CONV_W), x_prompt.dtype)
    zero_pool = jnp.zeros((b, POOL_HIST, POOL_W), x_prompt.dtype)
    sample_pool_valid = min(PAST_LEN, POOL_HIST)
    hp, hs = x_prompt, x_sample
    kp_l, vp_l, cp_l, pp_l = [], [], [], []
    ks_l, vs_l, cs_l, ps_l = [], [], [], []
    for li in range(DEPTH):
        wl = (norm_g[li], w_ffn_up[li], w_ffn_down[li], w_in[li], lam[li], subln_g[li], conv_w[li],
              pool_w[li], pool_scale[li], w_branch[li], w_out[li], w_ple_up[li], w_ple_gate[li])
        hp, kp, vp, cp, pp = encoder_layer(li, hp, p_prompt[li], pos_p, None, None,
                                           zero_conv, zero_pool, 0, *wl)
        hs, ks_, vs_, cs_, ps_ = encoder_layer(li, hs, p_sample[li], pos_s, cache_k[li], cache_v[li],
                                               state_conv[li], state_pool[li], sample_pool_valid, *wl)
        kp_l.append(kp); vp_l.append(vp); cp_l.append(cp); pp_l.append(pp)
        ks_l.append(ks_); vs_l.append(vs_); cs_l.append(cs_); ps_l.append(ps_)
    return (hp, hs,
            jnp.stack(kp_l), jnp.stack(vp_l), jnp.stack(cp_l), jnp.stack(pp_l),
            jnp.stack(ks_l), jnp.stack(vs_l), jnp.stack(cs_l), jnp.stack(ps_l))
```

```python
import functools
import math

import jax
import jax.numpy as jnp
import numpy as np
from jax import lax
from jax.experimental import pallas as pl
from jax.experimental.pallas import tpu as pltpu

D_MODEL = 1024
N_HEADS = 8
HEAD_DIM = 64
HEAD_W = 2 * HEAD_DIM
ATT_W = N_HEADS * HEAD_W
CHUNK = 64
CONV_W = 512
CONV_K = 3
POOL_W = 512
POOL_WINDOWS = (2, 4, 8, 16)
POOL_GW = POOL_W // len(POOL_WINDOWS)
POOL_HIST = max(POOL_WINDOWS) - 1
HIST_ROWS = 16
D_FF = 2816
PLE_DIM = 256
ROPE_THETA = 10000.0
RMS_EPS = 1e-6
N_BRANCH = 3

OFF_Q = 0
OFF_K = ATT_W
OFF_V = 2 * ATT_W
OFF_BCX = 3 * ATT_W
OFF_ZP = OFF_BCX + 3 * CONV_W
OFF_GATES = OFF_ZP + POOL_W
IN_COLS = OFF_GATES + N_BRANCH * D_MODEL

VMEM_LIMIT_BYTES = 56 * 1024 * 1024

FFN_TM = 512
FFN_TF = 256
INPROJ_TM = 256
ATT_T = 256

BF16 = jnp.bfloat16
F32 = jnp.float32


def _dot(a, b):
    return jnp.dot(a, b, preferred_element_type=F32)


def _dot_nt(a, b):
    return lax.dot_general(a, b, (((1,), (1,)), ((), ())), preferred_element_type=F32)


def _rms(x, g):
    ms = jnp.mean(x * x, axis=-1, keepdims=True)
    return x * lax.rsqrt(ms + RMS_EPS) * g


def _const_spec(shape):
    nd = len(shape)
    return pl.BlockSpec(shape, lambda *_: (0,) * nd, pipeline_mode=pl.Buffered(1))


def _params(sem):
    return pltpu.CompilerParams(dimension_semantics=sem, vmem_limit_bytes=VMEM_LIMIT_BYTES)


def _ffn_kernel(*refs, with_ple):
    if with_ple:
        (x_ref, gpre_ref, wup_ref, wdn_ref, gpost_ref,
         p_ref, gple_ref, wpg_ref, wpu_ref, gout_ref, o_ref) = refs
    else:
        x_ref, gpre_ref, wup_ref, wdn_ref, gpost_ref, o_ref = refs
    x = x_ref[...]
    xn = _rms(x, gpre_ref[...]).astype(BF16)
    acc = jnp.zeros(x.shape, F32)
    for c in range(D_FF // FFN_TF):
        lo = c * FFN_TF
        gate = _dot(xn, wup_ref[:, lo:lo + FFN_TF])
        up = _dot(xn, wup_ref[:, D_FF + lo:D_FF + lo + FFN_TF])
        act = (gate * jax.nn.sigmoid(gate) * up).astype(BF16)
        acc = acc + _dot(act, wdn_ref[lo:lo + FFN_TF, :])
    h = x + 0.5 * _rms(acc, gpost_ref[...])
    if with_ple:
        gate = jax.nn.sigmoid(_dot(_rms(h, gple_ref[...]).astype(BF16), wpg_ref[...]))
        emb = _dot(p_ref[...].astype(BF16), wpu_ref[...])
        h = h + _rms(emb * gate, gout_ref[...])
    o_ref[...] = h


def _ffn(x, g_pre, w_up, w_down, g_post, ple=None):
    t = x.shape[0]
    tm = min(FFN_TM, t)
    assert t % tm == 0
    row = lambda i: (i, 0)
    args = [x, g_pre, w_up, w_down, g_post]
    specs = [pl.BlockSpec((tm, D_MODEL), row), _const_spec((1, D_MODEL)), _const_spec(w_up.shape),
             _const_spec(w_down.shape), _const_spec((1, D_MODEL))]
    if ple is not None:
        p, g_ple, w_gate, w_ple_up, g_out = ple
        args += [p, g_ple, w_gate, w_ple_up, g_out]
        specs += [pl.BlockSpec((tm, PLE_DIM), row), _const_spec((1, D_MODEL)), _const_spec(w_gate.shape),
                  _const_spec(w_ple_up.shape), _const_spec((1, D_MODEL))]
    return pl.pallas_call(
        functools.partial(_ffn_kernel, with_ple=ple is not None),
        grid=(t // tm,),
        in_specs=specs,
        out_specs=pl.BlockSpec((tm, D_MODEL), row),
        out_shape=jax.ShapeDtypeStruct((t, D_MODEL), F32),
        compiler_params=_params(("parallel",)),
        name="ffn_ple" if ple is not None else "ffn",
    )(*args)


def _rope_cols(x, cos, sin_signed, c):
    xc = x[:, c * HEAD_W:(c + 1) * HEAD_W]
    lane = lax.broadcasted_iota(jnp.int32, xc.shape, 1)
    first_half = (lane % HEAD_DIM) < HEAD_DIM // 2
    rot = jnp.where(first_half, pltpu.roll(xc, HEAD_W - HEAD_DIM // 2, 1), pltpu.roll(xc, HEAD_DIM // 2, 1))
    return xc * cos + rot * sin_signed


def _inproj_kernel(*refs, nseq, seq_len, has_state, hist_valid):
    if has_state:
        (h_ref, g_ref, win_ref, cos_ref, sin_ref, convw_ref, poolw_ref, pscale_ref, wb_ref, chist_ref, phist_ref,
         q_ref, k_ref, kb_ref, v_ref, vb_ref, mcp_ref, g0_ref, cnew_ref, pnew_ref, zc_scr, zp_scr) = refs
    else:
        (h_ref, g_ref, win_ref, cos_ref, sin_ref, convw_ref, poolw_ref, pscale_ref, wb_ref,
         q_ref, k_ref, kb_ref, v_ref, vb_ref, mcp_ref, g0_ref, cnew_ref, pnew_ref, zc_scr, zp_scr) = refs
    t = pl.program_id(1)
    stride = seq_len + HIST_ROWS
    u = _rms(h_ref[...], g_ref[...]).astype(BF16)
    cos = cos_ref[...]
    sin_signed = sin_ref[...]

    q = _dot(u, win_ref[:, OFF_Q:OFF_Q + ATT_W])
    for c in range(N_HEADS):
        cols = slice(c * HEAD_W, (c + 1) * HEAD_W)
        q_ref[:, cols] = (_rope_cols(q, cos, sin_signed, c) * (HEAD_DIM ** -0.5)).astype(BF16)
    k = _dot(u, win_ref[:, OFF_K:OFF_K + ATT_W])
    for c in range(N_HEADS):
        cols = slice(c * HEAD_W, (c + 1) * HEAD_W)
        kr = _rope_cols(k, cos, sin_signed, c)
        k_ref[:, cols] = kr
        kb_ref[:, cols] = kr.astype(BF16)
    v = _dot(u, win_ref[:, OFF_V:OFF_V + ATT_W])
    v_ref[...] = v
    vb_ref[...] = v.astype(BF16)

    bcx = _dot(u, win_ref[:, OFF_BCX:OFF_BCX + 3 * CONV_W])
    bg = bcx[:, :CONV_W]
    z = bcx[:, CONV_W:2 * CONV_W] * bcx[:, 2 * CONV_W:]
    zp = _dot(u, win_ref[:, OFF_ZP:OFF_ZP + POOL_W])

    if has_state:
        for s in range(nseq):
            zc_scr[s * stride + HIST_ROWS - (CONV_K - 1):s * stride + HIST_ROWS, :] = chist_ref[s]
            zp_scr[s * stride + HIST_ROWS - POOL_HIST:s * stride + HIST_ROWS, :] = phist_ref[s]
    else:
        @pl.when(t == 0)
        def _():
            zc_scr[0:HIST_ROWS, :] = jnp.zeros((HIST_ROWS, CONV_W), F32)
            zp_scr[0:HIST_ROWS, :] = jnp.zeros((HIST_ROWS, POOL_W), F32)
    for s in range(nseq):
        base = s * stride + HIST_ROWS
        zc_scr[base:base + seq_len, :] = z[s * seq_len:(s + 1) * seq_len]
        zp_scr[base:base + seq_len, :] = zp[s * seq_len:(s + 1) * seq_len]

    convw = convw_ref[...]
    y_conv = []
    for s in range(nseq):
        base = s * stride + HIST_ROWS
        y = convw[CONV_K - 1:CONV_K] * z[s * seq_len:(s + 1) * seq_len]
        for j in range(CONV_K - 1):
            off = base - (CONV_K - 1) + j
            y = y + convw[j:j + 1] * zc_scr[off:off + seq_len, :]
        y_conv.append(bg[s * seq_len:(s + 1) * seq_len] * y)
        cnew_ref[s] = zc_scr[base + seq_len - (CONV_K - 1):base + seq_len, :]
    y_conv = jnp.concatenate(y_conv, axis=0) if nseq > 1 else y_conv[0]

    pos = t * seq_len + lax.broadcasted_iota(jnp.int32, (seq_len, POOL_GW), 0)
    y_pool = []
    for g, w in enumerate(POOL_WINDOWS):
        cols = slice(g * POOL_GW, (g + 1) * POOL_GW)
        cnt = jnp.minimum(pos + 1 + hist_valid, w).astype(F32)
        d = []
        for s in range(nseq):
            base = s * stride + HIST_ROWS
            cur = zp[s * seq_len:(s + 1) * seq_len, cols]
            wsum = cur
            for j in range(1, w):
                wsum = wsum + zp_scr[base - j:base - j + seq_len, cols]
            d.append(wsum / cnt - cur)
        d = jnp.concatenate(d, axis=0) if nseq > 1 else d[0]
        y_pool.append(_dot(d.astype(BF16), poolw_ref[g]) * pscale_ref[:, cols])
    for s in range(nseq):
        base = s * stride + HIST_ROWS
        pnew_ref[s] = zp_scr[base + seq_len - POOL_HIST:base + seq_len, :]
    if not has_state:
        zc_scr[0:HIST_ROWS, :] = zc_scr[seq_len:seq_len + HIST_ROWS, :]
        zp_scr[0:HIST_ROWS, :] = zp_scr[seq_len:seq_len + HIST_ROWS, :]

    gates = _dot(u, win_ref[:, OFF_GATES:OFF_GATES + N_BRANCH * D_MODEL])
    g0_ref[...] = jax.nn.sigmoid(gates[:, :D_MODEL])
    merged = jax.nn.sigmoid(gates[:, D_MODEL:2 * D_MODEL]) * _dot(y_conv.astype(BF16), wb_ref[:CONV_W, :])
    for g in range(len(POOL_WINDOWS)):
        rows = slice(CONV_W + g * POOL_GW, CONV_W + (g + 1) * POOL_GW)
        part = _dot(y_pool[g].astype(BF16), wb_ref[rows, :])
        pool_proj = part if g == 0 else pool_proj + part
    mcp_ref[...] = merged + jax.nn.sigmoid(gates[:, 2 * D_MODEL:]) * pool_proj


def _inproj(h, g, w_in, cos, sin_signed, conv_w, pool_w, pool_scale, w_branch_cp, state, *, seq_len, hist_valid):
    nb, tb, _ = h.shape
    has_state = state is not None
    tm = tb if has_state else min(INPROJ_TM, tb)
    tile_seq_len = seq_len if has_state else tm
    nseq = tm // tile_seq_len
    assert tb % tm == 0 and tm % tile_seq_len == 0
    tile = lambda w: pl.BlockSpec((None, tm, w), lambda b, t: (b, t, 0))
    per_seq = lambda r, w: pl.BlockSpec((nseq, r, w), lambda b, t: (b, 0, 0))
    args = [h, g, w_in, cos, sin_signed, conv_w, pool_w, pool_scale, w_branch_cp]
    specs = [tile(D_MODEL), _const_spec((1, D_MODEL)), _const_spec(w_in.shape),
             pl.BlockSpec((tm, HEAD_W), lambda b, t: (t, 0)), pl.BlockSpec((tm, HEAD_W), lambda b, t: (t, 0)),
             _const_spec(conv_w.shape), _const_spec(pool_w.shape), _const_spec((1, POOL_W)),
             _const_spec(w_branch_cp.shape)]
    if has_state:
        args += list(state)
        specs += [per_seq(CONV_K - 1, CONV_W), per_seq(POOL_HIST, POOL_W)]
    n_total_seq = nb * nseq
    out_shape = (
        jax.ShapeDtypeStruct((nb, tb, ATT_W), BF16),
        jax.ShapeDtypeStruct((nb, tb, ATT_W), F32),
        jax.ShapeDtypeStruct((nb, tb, ATT_W), BF16),
        jax.ShapeDtypeStruct((nb, tb, ATT_W), F32),
        jax.ShapeDtypeStruct((nb, tb, ATT_W), BF16),
        jax.ShapeDtypeStruct((nb, tb, D_MODEL), F32),
        jax.ShapeDtypeStruct((nb, tb, D_MODEL), F32),
        jax.ShapeDtypeStruct((n_total_seq, CONV_K - 1, CONV_W), F32),
        jax.ShapeDtypeStruct((n_total_seq, POOL_HIST, POOL_W), F32),
    )
    out_specs = (tile(ATT_W), tile(ATT_W), tile(ATT_W), tile(ATT_W), tile(ATT_W), tile(D_MODEL), tile(D_MODEL),
                 per_seq(CONV_K - 1, CONV_W), per_seq(POOL_HIST, POOL_W))
    scr_rows = nseq * (tile_seq_len + HIST_ROWS)
    return pl.pallas_call(
        functools.partial(_inproj_kernel, nseq=nseq, seq_len=tile_seq_len, has_state=has_state, hist_valid=hist_valid),
        grid=(nb, tb // tm),
        in_specs=specs,
        out_specs=out_specs,
        out_shape=out_shape,
        scratch_shapes=[pltpu.VMEM((scr_rows, CONV_W), F32), pltpu.VMEM((scr_rows, POOL_W), F32)],
        compiler_params=_params(("arbitrary", "arbitrary")),
        name="inproj_state" if has_state else "inproj",
    )(*args)


def _split_maps(kh):
    lane = lax.broadcasted_iota(jnp.int32, kh.shape, 1)
    zero = jnp.zeros_like(kh)
    return jnp.where(lane < HEAD_DIM, kh, zero), jnp.where(lane >= HEAD_DIM, kh, zero)


def _lam_full(lam_ref, consts_ref):
    lam = lam_ref[...]
    e1 = jnp.exp(jnp.sum(lam[0:1] * lam[1:2], axis=-1, keepdims=True))
    e2 = jnp.exp(jnp.sum(lam[2:3] * lam[3:4], axis=-1, keepdims=True))
    return e1 - e2 + consts_ref[0:1, :]


def _merge_out(heads, lam_ref, consts_ref, subg_ref, wba_ref, wout_ref, gpost_ref, g0_ref, mcp_ref, h_ref, o_ref):
    lam = _lam_full(lam_ref, consts_ref)
    out_scale = consts_ref[1:2, :]
    att = []
    for hd, (o1, o2) in enumerate(heads):
        cols = slice(hd * HEAD_W, (hd + 1) * HEAD_W)
        att.append((_rms(o1 - lam * o2, subg_ref[:, cols]) * out_scale).astype(BF16))
    att = jnp.concatenate(att, axis=-1)
    merged = g0_ref[...] * _dot(att, wba_ref[...]) + mcp_ref[...]
    out = _dot(merged.astype(BF16), wout_ref[...])
    o_ref[...] = h_ref[...] + _rms(out, gpost_ref[...])


def _attn_prompt_kernel(qi_tab, kj_tab, q_ref, k_ref, v_ref, g0_ref, mcp_ref, h_ref, lam_ref, consts_ref, subg_ref,
                        wba_ref, wout_ref, gpost_ref, o_ref, m_scr, l_scr, acc_scr):
    p = pl.program_id(1)
    qi = qi_tab[p]
    kj = kj_tab[p]
    tq, tk = q_ref.shape[0], k_ref.shape[0]

    @pl.when(kj == 0)
    def _():
        m_scr[...] = jnp.full(m_scr.shape, -jnp.inf, F32)
        l_scr[...] = jnp.zeros(l_scr.shape, F32)
        acc_scr[...] = jnp.zeros(acc_scr.shape, F32)

    def step(diagonal):
        if diagonal:
            row = lax.broadcasted_iota(jnp.int32, (tq, tk), 0)
            col = lax.broadcasted_iota(jnp.int32, (tq, tk), 1)
            visible = (row // CHUNK) >= (col // CHUNK)
        for hd in range(N_HEADS):
            cols = slice(hd * HEAD_W, (hd + 1) * HEAD_W)
            qh = q_ref[:, cols]
            vh = v_ref[:, cols]
            for mp, km in enumerate(_split_maps(k_ref[:, cols])):
                idx = 2 * hd + mp
                s = _dot_nt(qh, km)
                if diagonal:
                    s = jnp.where(visible, s, -jnp.inf)
                m_prev = m_scr[idx]
                m_new = jnp.maximum(m_prev, jnp.max(s, axis=-1, keepdims=True))
                alpha = jnp.exp(m_prev - m_new)
                pr = jnp.exp(s - jnp.concatenate([m_new] * (tk // HEAD_W), axis=-1))
                l_scr[idx] = alpha * l_scr[idx] + jnp.sum(pr, axis=-1, keepdims=True)
                m_scr[idx] = m_new
                acc_scr[mp, :, cols] = alpha * acc_scr[mp, :, cols] + _dot(pr.astype(BF16), vh)

    @pl.when(kj < qi)
    def _():
        step(False)

    @pl.when(kj == qi)
    def _():
        step(True)
        heads = []
        for hd in range(N_HEADS):
            cols = slice(hd * HEAD_W, (hd + 1) * HEAD_W)
            heads.append((acc_scr[0, :, cols] / l_scr[2 * hd], acc_scr[1, :, cols] / l_scr[2 * hd + 1]))
        _merge_out(heads, lam_ref, consts_ref, subg_ref, wba_ref, wout_ref, gpost_ref, g0_ref, mcp_ref, h_ref, o_ref)


def _attn_prompt(q, kb, vb, g0, mcp, h, lam, consts, subg, w_branch_att, w_out, g_post):
    b, s, _ = q.shape
    t = min(ATT_T, s)
    assert s % t == 0 and t % CHUNK == 0
    n = s // t
    pairs = [(i, j) for i in range(n) for j in range(i + 1)]
    qi_tab = jnp.asarray(np.array([p[0] for p in pairs], np.int32))
    kj_tab = jnp.asarray(np.array([p[1] for p in pairs], np.int32))
    q_tile = lambda w: pl.BlockSpec((None, t, w), lambda bi, p, qi, kj: (bi, qi[p], 0))
    k_tile = lambda w: pl.BlockSpec((None, t, w), lambda bi, p, qi, kj: (bi, kj[p], 0))
    grid_spec = pltpu.PrefetchScalarGridSpec(
        num_scalar_prefetch=2,
        grid=(b, len(pairs)),
        in_specs=[q_tile(ATT_W), k_tile(ATT_W), k_tile(ATT_W), q_tile(D_MODEL), q_tile(D_MODEL), q_tile(D_MODEL),
                  _const_spec(lam.shape), _const_spec(consts.shape), _const_spec((1, ATT_W)),
                  _const_spec(w_branch_att.shape), _const_spec(w_out.shape), _const_spec((1, D_MODEL))],
        out_specs=q_tile(D_MODEL),
        scratch_shapes=[pltpu.VMEM((2 * N_HEADS, t, HEAD_W), F32), pltpu.VMEM((2 * N_HEADS, t, HEAD_W), F32),
                        pltpu.VMEM((2, t, ATT_W), F32)],
    )
    return pl.pallas_call(
        _attn_prompt_kernel,
        grid_spec=grid_spec,
        out_shape=jax.ShapeDtypeStruct((b, s, D_MODEL), F32),
        compiler_params=_params(("arbitrary", "arbitrary")),
        name="attn_prompt",
    )(qi_tab, kj_tab, q, kb, vb, g0, mcp, h, lam, consts, subg, w_branch_att, w_out, g_post)


def _attn_sample_kernel(q_ref, kc_ref, vc_ref, kn_ref, vn_ref, g0_ref, mcp_ref, h_ref, lam_ref, consts_ref, subg_ref,
                        wba_ref, wout_ref, gpost_ref, o_ref):
    heads = []
    for hd in range(N_HEADS):
        cols = slice(hd * HEAD_W, (hd + 1) * HEAD_W)
        qh = q_ref[:, cols]
        vch = vc_ref[:, cols].astype(BF16)
        vnh = vn_ref[:, cols]
        outs = []
        for kcm, knm in zip(_split_maps(kc_ref[:, cols].astype(BF16)), _split_maps(kn_ref[:, cols])):
            sc = _dot_nt(qh, kcm)
            sn = _dot_nt(qh, knm)
            m = jnp.maximum(jnp.max(sc, axis=-1, keepdims=True), jnp.max(sn, axis=-1, keepdims=True))
            pc = jnp.exp(sc - m)
            pn = jnp.exp(sn - m)
            l = jnp.sum(pc, axis=-1, keepdims=True) + jnp.sum(pn, axis=-1, keepdims=True)
            outs.append((_dot(pc.astype(BF16), vch) + _dot(pn.astype(BF16), vnh)) / l)
        heads.append(tuple(outs))
    _merge_out(heads, lam_ref, consts_ref, subg_ref, wba_ref, wout_ref, gpost_ref, g0_ref, mcp_ref, h_ref, o_ref)


def _attn_sample(q, k_cache, v_cache, kb, vb, g0, mcp, h, lam, consts, subg, w_branch_att, w_out, g_post):
    b, t, _ = q.shape
    past = k_cache.shape[1]
    seq = lambda r, w: pl.BlockSpec((None, r, w), lambda bi: (bi, 0, 0))
    return pl.pallas_call(
        _attn_sample_kernel,
        grid=(b,),
        in_specs=[seq(t, ATT_W), seq(past, ATT_W), seq(past, ATT_W), seq(t, ATT_W), seq(t, ATT_W),
                  seq(t, D_MODEL), seq(t, D_MODEL), seq(t, D_MODEL),
                  _const_spec(lam.shape), _const_spec(consts.shape), _const_spec((1, ATT_W)),
                  _const_spec(w_branch_att.shape), _const_spec(w_out.shape), _const_spec((1, D_MODEL))],
        out_specs=seq(t, D_MODEL),
        out_shape=jax.ShapeDtypeStruct((b, t, D_MODEL), F32),
        compiler_params=_params(("parallel",)),
        name="attn_sample",
    )(q, k_cache, v_cache, kb, vb, g0, mcp, h, lam, consts, subg, w_branch_att, w_out, g_post)


def _rope_tables(pos):
    half = HEAD_DIM // 2
    inv = ROPE_THETA ** (-jnp.arange(half, dtype=F32) / half)
    ang = pos.astype(F32)[:, None] * inv[None, :]
    cos, sin = jnp.cos(ang), jnp.sin(ang)
    return jnp.tile(cos, (1, 4)), jnp.tile(jnp.concatenate([-sin, sin], axis=-1), (1, 2))


def kernel(x_prompt, x_sample, cache_k, cache_v, state_conv, state_pool, p_prompt, p_sample, norm_g, w_ffn_up,
           w_ffn_down, w_in, lam, subln_g, conv_w, pool_w, pool_scale, w_branch, w_out, w_ple_up, w_ple_gate):
    b, s, _ = x_prompt.shape
    db, ds, _ = x_sample.shape
    depth = norm_g.shape[0]
    past = cache_k.shape[2]
    sample_pool_valid = min(past, POOL_HIST)
    sample_groups = 2 if db % 2 == 0 else 1
    cos_p, sin_p = _rope_tables(jnp.arange(s, dtype=jnp.int32))
    cos_s, sin_s = _rope_tables(past + jnp.arange(ds, dtype=jnp.int32))
    cos_s = jnp.tile(cos_s, (db // sample_groups, 1))
    sin_s = jnp.tile(sin_s, (db // sample_groups, 1))

    hp = x_prompt.reshape(b * s, D_MODEL)
    hs = x_sample.reshape(db * ds, D_MODEL)
    outs = [[] for _ in range(8)]
    for li in range(depth):
        g = norm_g[li][:, None, :]
        wup = w_ffn_up[li].astype(BF16)
        wdn = w_ffn_down[li].astype(BF16)
        win = w_in[li].astype(BF16)
        wb = w_branch[li].astype(BF16)
        wout = w_out[li].astype(BF16)
        wpu = w_ple_up[li].astype(BF16)
        wpg = w_ple_gate[li].astype(BF16)
        poolw = pool_w[li].astype(BF16)
        pscale = pool_scale[li][None, :]
        subg = subln_g[li][None, :]
        lam_init = 0.8 - 0.6 * math.exp(-0.3 * li)
        consts = jnp.stack([jnp.full((HEAD_W,), lam_init, F32), jnp.full((HEAD_W,), 1.0 - lam_init, F32)])

        hp = _ffn(hp, g[0], wup[0], wdn[0], g[1])
        hs = _ffn(hs, g[0], wup[0], wdn[0], g[1])

        qp, kp, kpb, vp, vpb, mcp_p, g0_p, cnew_p, pnew_p = _inproj(
            hp.reshape(b, s, D_MODEL), g[2], win, cos_p, sin_p, conv_w[li], poolw, pscale, wb[ATT_W:], None,
            seq_len=s, hist_valid=0)
        qs, ks, ksb, vs, vsb, mcp_s, g0_s, cnew_s, pnew_s = _inproj(
            hs.reshape(sample_groups, -1, D_MODEL), g[2], win, cos_s, sin_s, conv_w[li], poolw, pscale, wb[ATT_W:],
            (state_conv[li], state_pool[li]), seq_len=ds, hist_valid=sample_pool_valid)

        hp = _attn_prompt(qp, kpb, vpb, g0_p, mcp_p, hp.reshape(b, s, D_MODEL), lam[li], consts, subg,
                          wb[:ATT_W], wout, g[3]).reshape(b * s, D_MODEL)
        per_seq = lambda a: a.reshape(db, ds, a.shape[-1])
        hs = _attn_sample(per_seq(qs), cache_k[li].reshape(db, past, ATT_W), cache_v[li].reshape(db, past, ATT_W),
                          per_seq(ksb), per_seq(vsb), per_seq(g0_s), per_seq(mcp_s), per_seq(hs), lam[li], consts,
                          subg, wb[:ATT_W], wout, g[3]).reshape(db * ds, D_MODEL)

        hp = _ffn(hp, g[4], wup[1], wdn[1], g[5],
                  ple=(p_prompt[li].reshape(b * s, PLE_DIM), g[6], wpg, wpu, g[7]))
        hs = _ffn(hs, g[4], wup[1], wdn[1], g[5],
                  ple=(p_sample[li].reshape(db * ds, PLE_DIM), g[6], wpg, wpu, g[7]))

        for lst, a in zip(outs, (kp.reshape(b, s, 2 * N_HEADS, HEAD_DIM), vp.reshape(b, s, N_HEADS, HEAD_W),
                                 cnew_p, pnew_p, ks.reshape(db, ds, 2 * N_HEADS, HEAD_DIM),
                                 vs.reshape(db, ds, N_HEADS, HEAD_W), cnew_s, pnew_s)):
            lst.append(a)
    return (hp.reshape(b, s, D_MODEL), hs.reshape(db, ds, D_MODEL)) + tuple(jnp.stack(lst) for lst in outs)
```
